```python
import math
import jax, jax.numpy as jnp
from jax import lax
import numpy as np

D_MODEL = 1024
BATCH = 8
SEQ = 2048
DEPTH = 4

HG_HEADS = 4
HG_DK = 128
HG_DV = 128
HG_K = HG_HEADS * HG_DK
HG_V = HG_HEADS * HG_DV
HG_CHUNK = 64
SG_GROUPS = 4
SG_WIDTH = 512
SG_GROUP_DIM = SG_WIDTH // SG_GROUPS
SG_CHUNK = 128
D_FF = 2816
ALPHA = (2 * DEPTH) ** 0.25
BETA = (8 * DEPTH) ** -0.25
LN_EPS = 1e-5
RMS_EPS = 1e-6
D_IN = 2 * HG_K + 2 * HG_V + 2 * SG_WIDTH + 2 * D_MODEL
IN_SPLITS = (HG_K,
             2 * HG_K,
             2 * HG_K + HG_V,
             2 * HG_K + 2 * HG_V,
             2 * HG_K + 2 * HG_V + SG_WIDTH,
             2 * HG_K + 2 * HG_V + 2 * SG_WIDTH,
             2 * HG_K + 2 * HG_V + 2 * SG_WIDTH + D_MODEL)

kernel_name = "hgrn2_sgu_macaron_deepnorm_hybrid"


def layer_norm(x, g, b):
    xf = x.astype(jnp.float32)
    mu = jnp.mean(xf, axis=-1, keepdims=True)
    var = jnp.mean(jnp.square(xf - mu), axis=-1, keepdims=True)
    y = (xf - mu) * lax.rsqrt(var + LN_EPS) * g.astype(jnp.float32) + b.astype(jnp.float32)
    return y.astype(x.dtype)


def swiglu(x, w13, w2):
    a, b = jnp.split(x @ w13, 2, axis=-1)
    return (jax.nn.silu(a) * b) @ w2


def hgrn2_chunkwise(q, k, i, log_f):
    B, T, H, _ = q.shape
    n_chunks = T // HG_CHUNK

    def to_chunks(a):
        return a.reshape(B, n_chunks, HG_CHUNK, H, a.shape[-1]).transpose(1, 0, 3, 2, 4)

    causal = jnp.tril(jnp.ones((HG_CHUNK, HG_CHUNK), dtype=bool))[:, :, None]

    def step(S, inp):
        q_c, k_c, i_c, lf_c = inp
        b = jnp.cumsum(lf_c, axis=-2)
        o_inter = jnp.einsum('bhtk,bhkv->bhtv', q_c * jnp.exp(b), S)
        diff = b[:, :, :, None, :] - b[:, :, None, :, :]
        decay = jnp.where(causal, jnp.exp(jnp.where(causal, diff, 0.0)), 0.0)
        scores = jnp.einsum('bhtk,bhtsk,bhsk->bhts', q_c, decay, k_c)
        o_intra = jnp.einsum('bhts,bhsv->bhtv', scores, i_c)
        b_last = b[:, :, -1:, :]
        S_new = jnp.exp(b_last[:, :, 0, :])[..., None] * S + jnp.einsum(
            'bhsk,bhsv->bhkv', k_c * jnp.exp(b_last - b), i_c)
        return S_new, o_inter + o_intra

    S0 = jnp.zeros((B, H, q.shape[-1], i.shape[-1]), jnp.float32)
    _, o = lax.scan(step, S0, (to_chunks(q), to_chunks(k), to_chunks(i), to_chunks(log_f)))
    return o.transpose(1, 0, 3, 2, 4).reshape(B, T, H, i.shape[-1])


def mixer(x, w_in, lb, hg_norm_g, sg_ln_g, sg_ln_b, sg_ws, sg_bs, w_branch_a, w_branch_b, w_out):
    B, T, _ = x.shape
    zq, zf, zi, zg, zu, zv, za, zb = jnp.split(x @ w_in, IN_SPLITS, axis=-1)

    f32 = jnp.float32
    lbf = lb.astype(f32)
    zff = zf.astype(f32)
    f = lbf + (1.0 - lbf) * jax.nn.sigmoid(zff)
    log_f = jnp.log(f)
    k = (1.0 - lbf) * jax.nn.sigmoid(-zff)
    hs = lambda a, d: a.reshape(B, T, HG_HEADS, d)
    o = hgrn2_chunkwise(hs(zq.astype(f32), HG_DK), hs(k, HG_DK),
                        hs(zi.astype(f32), HG_DV), hs(log_f, HG_DK))
    o = o * lax.rsqrt(jnp.mean(jnp.square(o), axis=-1, keepdims=True) + RMS_EPS)
    o = o * hg_norm_g.astype(f32).reshape(HG_HEADS, HG_DV)
    o = o.reshape(B, T, HG_V).astype(x.dtype) * jax.nn.silu(zg)
    y_a = o @ w_branch_a

    u = jax.nn.gelu(zu)
    v = layer_norm(jax.nn.gelu(zv), sg_ln_g, sg_ln_b)
    n_chunks = T // SG_CHUNK
    v = v.reshape(B, n_chunks, SG_CHUNK, SG_GROUPS, SG_GROUP_DIM)
    ws = sg_ws * jnp.tril(jnp.ones((SG_CHUNK, SG_CHUNK), sg_ws.dtype))
    mixed = jnp.einsum('gts,bnsgc->bntgc', ws, v) + sg_bs.T[None, None, :, :, None]
    y_b = (u * mixed.reshape(B, T, SG_WIDTH)) @ w_branch_b

    y = jax.nn.sigmoid(za) * y_a + jax.nn.sigmoid(zb) * y_b
    return y @ w_out


def setup_inputs(seed: int = 0) -> dict:
    key = jax.random.key(seed)
    ks = jax.random.split(key, 20)
    n = jax.random.normal
    f32 = jnp.float32
    return {
        "x": n(ks[0], (BATCH, SEQ, D_MODEL), f32),
        "ffn1_w13": n(ks[1], (DEPTH, D_MODEL, 2 * D_FF), f32) * D_MODEL ** -0.5,
        "ffn1_w2": n(ks[2], (DEPTH, D_FF, D_MODEL), f32) * (D_FF ** -0.5 * BETA),
        "ffn2_w13": n(ks[3], (DEPTH, D_MODEL, 2 * D_FF), f32) * D_MODEL ** -0.5,
        "ffn2_w2": n(ks[4], (DEPTH, D_FF, D_MODEL), f32) * (D_FF ** -0.5 * BETA),
        "ln_g": 1.0 + 0.02 * n(ks[5], (DEPTH, 3, D_MODEL), f32),
        "ln_b": 0.02 * n(ks[6], (DEPTH, 3, D_MODEL), f32),
        "w_in": n(ks[7], (DEPTH, D_MODEL, D_IN), f32) * D_MODEL ** -0.5,
        "hg_lb": 0.5 * n(ks[8], (DEPTH, HG_K), f32),
        "hg_norm_g": 1.0 + 0.02 * n(ks[9], (DEPTH, HG_V), f32),
        "sg_ln_g": 1.0 + 0.02 * n(ks[10], (DEPTH, SG_WIDTH), f32),
        "sg_ln_b": 0.02 * n(ks[11], (DEPTH, SG_WIDTH), f32),
        "sg_ws": n(ks[12], (DEPTH, SG_GROUPS, SG_CHUNK, SG_CHUNK), f32) * SG_CHUNK ** -0.5,
        "sg_bs": 1.0 + 0.1 * n(ks[13], (DEPTH, SG_GROUPS, SG_CHUNK), f32),
        "w_branch_a": n(ks[14], (DEPTH, HG_V, D_MODEL), f32) * (HG_V ** -0.5 * BETA),
        "w_branch_b": n(ks[15], (DEPTH, SG_WIDTH, D_MODEL), f32) * (SG_WIDTH ** -0.5 * BETA),
        "w_out": n(ks[16], (DEPTH, D_MODEL, D_MODEL), f32) * (D_MODEL ** -0.5 * BETA),
    }


def reference(x, ffn1_w13, ffn1_w2, ffn2_w13, ffn2_w2, ln_g, ln_b, w_in, hg_lb, hg_norm_g,
              sg_ln_g, sg_ln_b, sg_ws, sg_bs, w_branch_a, w_branch_b, w_out):
    lb_soft = jax.nn.softmax(hg_lb.astype(jnp.float32), axis=0)
    lower_bounds = jnp.cumsum(lb_soft, axis=0) - lb_soft[0]
    for l in range(DEPTH):
        x = layer_norm(ALPHA * x + 0.5 * swiglu(x, ffn1_w13[l], ffn1_w2[l]), ln_g[l, 0], ln_b[l, 0])
        m = mixer(x, w_in[l], lower_bounds[l], hg_norm_g[l], sg_ln_g[l], sg_ln_b[l], sg_ws[l],
                  sg_bs[l], w_branch_a[l], w_branch_b[l], w_out[l])
        x = layer_norm(ALPHA * x + m, ln_g[l, 1], ln_b[l, 1])
        x = layer_norm(ALPHA * x + 0.5 * swiglu(x, ffn2_w13[l], ffn2_w2[l]), ln_g[l, 2], ln_b[l, 2])
    return x
```

```python
import functools

import jax
import jax.numpy as jnp
from jax import lax
from jax.experimental import pallas as pl
from jax.experimental.pallas import tpu as pltpu

D_MODEL = 1024
DEPTH = 4
HG_HEADS = 4
HG_DK = 128
HG_DV = 128
HG_K = HG_HEADS * HG_DK
HG_V = HG_HEADS * HG_DV
SG_GROUPS = 4
SG_WIDTH = 512
SG_GROUP_DIM = SG_WIDTH // SG_GROUPS
SG_CHUNK = 128
D_FF = 2816
ALPHA = (2 * DEPTH) ** 0.25
LN_EPS = 1e-5
RMS_EPS = 1e-6
D_IN = 2 * HG_K + 2 * HG_V + 2 * SG_WIDTH + 2 * D_MODEL

CHUNK = 128
FFN_ROWS = 512
FFN_COLS = 256
MIX_ROWS = 256
VMEM_LIMIT_BYTES = 56 * 1024 * 1024

F32 = jnp.float32
BF16 = jnp.bfloat16


def _layer_norm(y, g, b):
    mu = jnp.mean(y, axis=-1, keepdims=True)
    d = y - mu
    var = jnp.mean(d * d, axis=-1, keepdims=True)
    return d * lax.rsqrt(var + LN_EPS) * g + b


def _dot(a, b):
    return jnp.dot(a, b, preferred_element_type=F32)


def _dot_nt(a, b):
    return lax.dot_general(a, b, (((1,), (1,)), ((), ())), preferred_element_type=F32)


def _dot_tn(a, b):
    return lax.dot_general(a, b, (((0,), (0,)), ((), ())), preferred_element_type=F32)


def _ffn_kernel(x_ref, w13_ref, w2_ref, g_ref, b_ref, o_ref):
    x = x_ref[...]
    xb = x.astype(BF16)
    acc = jnp.zeros(x.shape, F32)
    for j in range(D_FF // FFN_COLS):
        lo = j * FFN_COLS
        a = _dot(xb, w13_ref[:, lo:lo + FFN_COLS])
        b = _dot(xb, w13_ref[:, D_FF + lo:D_FF + lo + FFN_COLS])
        h = (a * jax.nn.sigmoid(a)) * b
        acc = acc + _dot(h.astype(BF16), w2_ref[lo:lo + FFN_COLS, :])
    o_ref[...] = _layer_norm(ALPHA * x + 0.5 * acc, g_ref[...], b_ref[...])


def _ffn(x, w13, w2, ln_g, ln_b, layer, which):
    n = x.shape[0]
    const = dict(pipeline_mode=pl.Buffered(1))
    return pl.pallas_call(
        _ffn_kernel,
        grid=(n // FFN_ROWS,),
        in_specs=[
            pl.BlockSpec((FFN_ROWS, D_MODEL), lambda i: (i, 0)),
            pl.BlockSpec((None, D_MODEL, 2 * D_FF), lambda i: (layer, 0, 0), **const),
            pl.BlockSpec((None, D_FF, D_MODEL), lambda i: (layer, 0, 0), **const),
            pl.BlockSpec((None, None, 1, D_MODEL), lambda i: (layer, which, 0, 0), **const),
            pl.BlockSpec((None, None, 1, D_MODEL), lambda i: (layer, which, 0, 0), **const),
        ],
        out_specs=pl.BlockSpec((FFN_ROWS, D_MODEL), lambda i: (i, 0)),
        out_shape=jax.ShapeDtypeStruct((n, D_MODEL), F32),
        compiler_params=pltpu.CompilerParams(
            dimension_semantics=("arbitrary",), vmem_limit_bytes=VMEM_LIMIT_BYTES),
        name=f"ffn{which}_l{layer}",
    )(x, w13, w2, ln_g, ln_b)


def _level_exponent(h, b, lf, row):
    if h == 1:
        return jnp.where((row & 1) == 1, lf, 0.0)
    if h == 2:
        nxt = pltpu.roll(lf, CHUNK - 1, 0)
        prv = pltpu.roll(lf, 1, 0)
        r = row & 3
        return jnp.where(r == 0, nxt, jnp.where(r == 2, lf, jnp.where(r == 3, lf + prv, 0.0)))
    width = b.shape[1]
    pieces = []
    for m in range(CHUNK // (2 * h)):
        mid = m * 2 * h + h
        pieces.append(jnp.broadcast_to(b[mid - 1:mid, :], (2 * h, width)))
    ref = pieces[0] if len(pieces) == 1 else jnp.concatenate(pieces, axis=0)
    return -jnp.abs(b - ref)


def _hgrn_chunk(q, k, i_b, lf, st_ref, tri, row_k, row, col):
    lf_hi = lf.astype(BF16)
    lf_lo = (lf - lf_hi.astype(F32)).astype(BF16)
    b = _dot(tri, lf_hi) + _dot(tri, lf_lo)
    b_last = b[CHUNK - 1:CHUNK, :]
    q_in = (q * jnp.exp(b)).astype(BF16)
    k_out = (k * jnp.exp(b_last - b)).astype(BF16)
    decay_all = jnp.exp(b_last)

    levels = []
    h = CHUNK // 2
    while h >= 1:
        e = jnp.exp(_level_exponent(h, b, lf, row_k))
        upper = (row_k & h) != 0
        levels.append((h, jnp.where(upper, q * e, 0.0).astype(BF16),
                       jnp.where(upper, 0.0, k * e).astype(BF16)))
        h //= 2
    qk = q * k
    xor = row ^ col

    outs = []
    for hd in range(HG_HEADS):
        ks = slice(hd * HG_DK, (hd + 1) * HG_DK)
        vs = slice(hd * HG_DV, (hd + 1) * HG_DV)
        diag = jnp.sum(qk[:, ks], axis=-1, keepdims=True)
        scores = jnp.where(xor == 0, diag, 0.0)
        for (h, ql, kl) in levels:
            p = _dot_nt(ql[:, ks], kl[:, ks])
            scores = scores + (p if 2 * h == CHUNK else jnp.where(xor < 2 * h, p, 0.0))
        st = st_ref[hd]
        iv = i_b[:, vs]
        o = _dot_nt(q_in[:, ks], st.astype(BF16)) + _dot(scores.astype(BF16), iv)
        st_ref[hd] = st * decay_all[:, ks] + _dot_tn(iv, k_out[:, ks])
        outs.append(o)
    return outs


def _mixer_kernel(layer, x_ref, w_in_ref, hg_lb_ref, hg_g_ref, sg_g_ref, sg_b_ref, ws_ref, bsx_ref,
                  wa_ref, wb_ref, wo_ref, g_ref, b_ref, o_ref, st_ref, oa_ref, ub_ref):
    @pl.when(pl.program_id(1) == 0)
    def _():
        st_ref[...] = jnp.zeros(st_ref.shape, F32)

    x = x_ref[...]
    xb = x.astype(BF16)
    rows = x.shape[0]

    lbs = hg_lb_ref[...]
    ex = jnp.exp(lbs - jnp.max(lbs, axis=0, keepdims=True))
    soft = ex / jnp.sum(ex, axis=0, keepdims=True)
    lb = jnp.zeros((1, HG_K), F32)
    for j in range(1, layer + 1):
        lb = lb + soft[j:j + 1, :]

    def proj(lo, width):
        return _dot(xb, w_in_ref[:, lo:lo + width])

    zq = proj(0, HG_K)
    zf = proj(HG_K, HG_K)
    zi = proj(2 * HG_K, HG_V)
    zg = proj(2 * HG_K + HG_V, HG_V)

    f = lb + (1.0 - lb) * jax.nn.sigmoid(zf)
    lf = jnp.log(f)
    kk = (1.0 - lb) * jax.nn.sigmoid(-zf)
    i_b = zi.astype(BF16)
    gate = zg * jax.nn.sigmoid(zg)
    hg_g = hg_g_ref[...]

    row = lax.broadcasted_iota(jnp.int32, (CHUNK, CHUNK), 0)
    col = lax.broadcasted_iota(jnp.int32, (CHUNK, CHUNK), 1)
    tri = jnp.where(col <= row, 1.0, 0.0).astype(BF16)
    row_k = lax.broadcasted_iota(jnp.int32, (CHUNK, HG_K), 0)

    for c in range(rows // CHUNK):
        rs = slice(c * CHUNK, (c + 1) * CHUNK)
        outs = _hgrn_chunk(zq[rs], kk[rs], i_b[rs], lf[rs], st_ref, tri, row_k, row, col)
        for hd in range(HG_HEADS):
            vs = slice(hd * HG_DV, (hd + 1) * HG_DV)
            o = outs[hd]
            o = o * lax.rsqrt(jnp.mean(o * o, axis=-1, keepdims=True) + RMS_EPS) * hg_g[:, vs]
            oa_ref[rs, vs] = (o * gate[rs, vs]).astype(BF16)
    y_a = _dot(oa_ref[...], wa_ref[...])

    zu = proj(2 * HG_K + 2 * HG_V, SG_WIDTH)
    zv = proj(2 * HG_K + 2 * HG_V + SG_WIDTH, SG_WIDTH)
    u = jax.nn.gelu(zu)
    v = _layer_norm(jax.nn.gelu(zv), sg_g_ref[...], sg_b_ref[...]).astype(BF16)
    bsx = bsx_ref[...]
    for g in range(SG_GROUPS):
        gs = slice(g * SG_GROUP_DIM, (g + 1) * SG_GROUP_DIM)
        ws = jnp.where(col <= row, ws_ref[g], 0.0).astype(BF16)
        for c in range(rows // CHUNK):
            rs = slice(c * CHUNK, (c + 1) * CHUNK)
            mixed = _dot(ws, v[rs, gs]) + bsx[:, gs]
            ub_ref[rs, gs] = (u[rs, gs] * mixed).astype(BF16)
    y_b = _dot(ub_ref[...], wb_ref[...])

    za = proj(2 * HG_K + 2 * HG_V + 2 * SG_WIDTH, D_MODEL)
    zb = proj(2 * HG_K + 2 * HG_V + 2 * SG_WIDTH + D_MODEL, D_MODEL)
    y = jax.nn.sigmoid(za) * y_a + jax.nn.sigmoid(zb) * y_b
    m = _dot(y.astype(BF16), wo_ref[...])
    o_ref[...] = _layer_norm(ALPHA * x + m, g_ref[...], b_ref[...])


def _mixer(x, layer, w_in, hg_lb, hg_norm_g, sg_ln_g, sg_ln_b, sg_ws, bsx, w_a, w_b, w_out, ln_g, ln_b,
           batch, seq):
    n = x.shape[0]
    steps = seq // MIX_ROWS
    const = dict(pipeline_mode=pl.Buffered(1))

    def vec(width):
        return pl.BlockSpec((None, 1, width), lambda bi, ti: (layer, 0, 0), **const)

    def mat(r, c):
        return pl.BlockSpec((None, r, c), lambda bi, ti: (layer, 0, 0), **const)

    return pl.pallas_call(
        functools.partial(_mixer_kernel, layer),
        grid=(batch, steps),
        in_specs=[
            pl.BlockSpec((MIX_ROWS, D_MODEL), lambda bi, ti: (bi * steps + ti, 0)),
            mat(D_MODEL, D_IN),
            pl.BlockSpec((DEPTH, HG_K), lambda bi, ti: (0, 0), **const),
            vec(HG_V), vec(SG_WIDTH), vec(SG_WIDTH),
            pl.BlockSpec((None, SG_GROUPS, SG_CHUNK, SG_CHUNK), lambda bi, ti: (layer, 0, 0, 0), **const),
            mat(SG_CHUNK, SG_WIDTH),
            mat(HG_V, D_MODEL), mat(SG_WIDTH, D_MODEL), mat(D_MODEL, D_MODEL),
            pl.BlockSpec((None, None, 1, D_MODEL), lambda bi, ti: (layer, 1, 0, 0), **const),
            pl.BlockSpec((None, None, 1, D_MODEL), lambda bi, ti: (layer, 1, 0, 0), **const),
        ],
        out_specs=pl.BlockSpec((MIX_ROWS, D_MODEL), lambda bi, ti: (bi * steps + ti, 0)),
        out_shape=jax.ShapeDtypeStruct((n, D_MODEL), F32),
        scratch_shapes=[
            pltpu.VMEM((HG_HEADS, HG_DV, HG_DK), F32),
            pltpu.VMEM((MIX_ROWS, HG_V), BF16),
            pltpu.VMEM((MIX_ROWS, SG_WIDTH), BF16),
        ],
        compiler_params=pltpu.CompilerParams(
            dimension_semantics=("arbitrary", "arbitrary"), vmem_limit_bytes=VMEM_LIMIT_BYTES),
        name=f"mixer_l{layer}",
    )(x, w_in, hg_lb, hg_norm_g, sg_ln_g, sg_ln_b, sg_ws, bsx, w_a, w_b, w_out, ln_g, ln_b)


def kernel(x, ffn1_w13, ffn1_w2, ffn2_w13, ffn2_w2, ln_g, ln_b, w_in, hg_lb, hg_norm_g, sg_ln_g, sg_ln_b,
           sg_ws, sg_bs, w_branch_a, w_branch_b, w_out):
    batch, seq, d = x.shape
    h = x.reshape(batch * seq, d)
    bf = lambda w: w.astype(BF16)
    f1_w13, f1_w2, f2_w13, f2_w2 = bf(ffn1_w13), bf(ffn1_w2), bf(ffn2_w13), bf(ffn2_w2)
    w_in_b, w_a_b, w_b_b, w_out_b = bf(w_in), bf(w_branch_a), bf(w_branch_b), bf(w_out)
    ln_g4 = ln_g.reshape(DEPTH, 3, 1, D_MODEL)
    ln_b4 = ln_b.reshape(DEPTH, 3, 1, D_MODEL)
    hg_g3 = hg_norm_g.reshape(DEPTH, 1, HG_V)
    sg_g3 = sg_ln_g.reshape(DEPTH, 1, SG_WIDTH)
    sg_b3 = sg_ln_b.reshape(DEPTH, 1, SG_WIDTH)
    bsx = jnp.repeat(jnp.swapaxes(sg_bs, 1, 2), SG_GROUP_DIM, axis=2)
    for l in range(DEPTH):
        h = _ffn(h, f1_w13, f1_w2, ln_g4, ln_b4, l, 0)
        h = _mixer(h, l, w_in_b, hg_lb, hg_g3, sg_g3, sg_b3, sg_ws, bsx, w_a_b, w_b_b, w_out_b,
                   ln_g4, ln_b4, batch, seq)
        h = _ffn(h, f2_w13, f2_w2, ln_g4, ln_b4, l, 2)
    return h.reshape(batch, seq, d)
```

```python
import functools

import jax
import jax.numpy as jnp
from jax import lax
from jax.experimental import pallas as pl
from jax.experimental.pallas import tpu as pltpu

D_MODEL = 1024
DEPTH = 4
HG_HEADS = 4
HG_DK = 128
HG_DV = 128
HG_K = HG_HEADS * HG_DK
HG_V = HG_HEADS * HG_DV
SG_GROUPS = 4
SG_WIDTH = 512
SG_GROUP_DIM = SG_WIDTH // SG_GROUPS
SG_CHUNK = 128
D_FF = 2816
ALPHA = (2 * DEPTH) ** 0.25
LN_EPS = 1e-5
RMS_EPS = 1e-6
D_IN = 2 * HG_K + 2 * HG_V + 2 * SG_WIDTH + 2 * D_MODEL
LOG2_E = 1.4426950408889634

CHUNK = 128
FFN_ROWS = 512
FFN_COLS = 256
FFN_NORM_ROWS = 64
MIX_ROWS = 512
PROJ_COLS = 256
F32_COLS = 2 * HG_K
VMEM_LIMIT_BYTES = 56 * 1024 * 1024

F32 = jnp.float32
BF16 = jnp.bfloat16


def _layer_norm(y, g, b):
    mu = jnp.mean(y, axis=-1, keepdims=True)
    d = y - mu
    var = jnp.mean(d * d, axis=-1, keepdims=True)
    return d * lax.rsqrt(var + LN_EPS) * g + b


def _dot(a, b):
    return jnp.dot(a, b, preferred_element_type=F32)


def _dot_nt(a, b):
    return lax.dot_general(a, b, (((1,), (1,)), ((), ())), preferred_element_type=F32)


def _dot_tn(a, b):
    return lax.dot_general(a, b, (((0,), (0,)), ((), ())), preferred_element_type=F32)


def _zero_after_stores(ref, lanes, opaque_zero):
    tile_rows = 8 * (4 // jnp.dtype(ref.dtype).itemsize)
    row = pl.multiple_of(opaque_zero * tile_rows, tile_rows)
    bits = pltpu.bitcast(ref[pl.ds(row, tile_rows), lanes].astype(F32), jnp.uint32)
    acc = bits[0:8, 0:128]
    for r in range(tile_rows // 8):
        for t in range(bits.shape[1] // 128):
            if r or t:
                acc = acc | bits[8 * r:8 * r + 8, t * 128:(t + 1) * 128]
    return pltpu.bitcast((acc >> 16) >> 16, F32)


def _add_zero(v, zero):
    if zero is None:
        return v
    z = zero if v.shape[1] == 128 else jnp.concatenate([zero] * (v.shape[1] // 128), axis=1)
    return jnp.concatenate([v[:8] + z, v[8:]], axis=0)


def _ffn_kernel(n_tiles, zero_ref, x_ref, w13_ref, w2_ref, g_ref, b_ref, o_ref, y_ref):
    i = pl.program_id(0)
    n_chunks = D_FF // FFN_COLS

    def norm_rows(rs):
        o_ref[rs, :] = _layer_norm(y_ref[rs, :], g_ref[...], b_ref[...])

    @pl.when(i == 0)
    def _():
        y_ref[...] = jnp.zeros(y_ref.shape, F32)

    @pl.when(i < n_tiles)
    def _():
        x = x_ref[...]
        xb = x.astype(BF16)
        acc = jnp.zeros(x.shape, F32)
        after_norm = None
        for j in range(n_chunks):
            lo = j * FFN_COLS
            a = _dot(xb, w13_ref[:, lo:lo + FFN_COLS])
            b = _dot(xb, w13_ref[:, D_FF + lo:D_FF + lo + FFN_COLS])
            h = (a * jax.nn.sigmoid(a)) * b
            h = _add_zero(h, after_norm)
            after_norm = None
            acc = acc + _dot(h.astype(BF16), w2_ref[lo:lo + FFN_COLS, :])
            if 1 <= j <= FFN_ROWS // FFN_NORM_ROWS:
                norm_rows(slice((j - 1) * FFN_NORM_ROWS, j * FFN_NORM_ROWS))
                after_norm = _zero_after_stores(o_ref, slice(None), zero_ref[0])
        y_ref[...] = ALPHA * x + 0.5 * acc

    @pl.when(i == n_tiles)
    def _():
        norm_rows(slice(None))


def _ffn(x, w13, w2, ln_g, ln_b, layer, which):
    n = x.shape[0]
    n_tiles = n // FFN_ROWS
    const = dict(pipeline_mode=pl.Buffered(1))
    return pl.pallas_call(
        functools.partial(_ffn_kernel, n_tiles),
        grid=(n_tiles + 1,),
        in_specs=[
            pl.BlockSpec(memory_space=pltpu.SMEM),
            pl.BlockSpec((FFN_ROWS, D_MODEL), lambda i: (jnp.minimum(i, n_tiles - 1), 0)),
            pl.BlockSpec((None, D_MODEL, 2 * D_FF), lambda i: (layer, 0, 0), **const),
            pl.BlockSpec((None, D_FF, D_MODEL), lambda i: (layer, 0, 0), **const),
            pl.BlockSpec((None, None, 1, D_MODEL), lambda i: (layer, which, 0, 0), **const),
            pl.BlockSpec((None, None, 1, D_MODEL), lambda i: (layer, which, 0, 0), **const),
        ],
        out_specs=pl.BlockSpec((FFN_ROWS, D_MODEL), lambda i: (jnp.maximum(i - 1, 0), 0)),
        out_shape=jax.ShapeDtypeStruct((n, D_MODEL), F32),
        scratch_shapes=[pltpu.VMEM((FFN_ROWS, D_MODEL), F32)],
        compiler_params=pltpu.CompilerParams(
            dimension_semantics=("arbitrary",), vmem_limit_bytes=VMEM_LIMIT_BYTES),
        name=f"ffn{which}_l{layer}",
    )(jnp.zeros((1,), jnp.int32), x, w13, w2, ln_g, ln_b)


def _small_level_exponent(h, b2, lf2, row):
    if h == 1:
        return jnp.where((row & 1) == 1, lf2, 0.0)
    if h == 2:
        nxt = pltpu.roll(lf2, CHUNK - 1, 0)
        prv = pltpu.roll(lf2, 1, 0)
        r = row & 3
        return jnp.where(r == 0, nxt, jnp.where(r == 2, lf2, jnp.where(r == 3, lf2 + prv, 0.0)))
    width = b2.shape[1]
    pieces = []
    for m in range(CHUNK // (2 * h)):
        mid = m * 2 * h + h
        pieces.append(jnp.broadcast_to(b2[mid - 1:mid, :], (2 * h, width)))
    return -jnp.abs(b2 - jnp.concatenate(pieces, axis=0))


def _hgrn_chunk(q, k, i_b, lf2, st_ref, tri, row_k, row, col):
    width = q.shape[1]
    lf_hi = lf2.astype(BF16)
    lf_lo = (lf2 - lf_hi.astype(F32)).astype(BF16)
    b2 = _dot(tri, lf_hi) + _dot(tri, lf_lo)
    b_last = b2[CHUNK - 1:CHUNK, :]
    q_in = (q * jnp.exp2(b2)).astype(BF16)
    k_out = (k * jnp.exp2(b_last - b2)).astype(BF16)
    decay_all = jnp.exp2(b_last)

    big_levels = []
    h = CHUNK // 2
    while h >= 8:
        q_parts, k_parts = [], []
        for m in range(CHUNK // (2 * h)):
            base, mid, top = m * 2 * h, m * 2 * h + h, (m + 1) * 2 * h
            ref = b2[mid - 1:mid, :]
            k_parts.append(k[base:mid] * jnp.exp2(ref - b2[base:mid]))
            k_parts.append(jnp.zeros((h, width), F32))
            q_parts.append(q[mid:top] * jnp.exp2(b2[mid:top] - ref))
        big_levels.append((h, jnp.concatenate(q_parts, axis=0).astype(BF16) if len(q_parts) > 1
                           else q_parts[0].astype(BF16),
                           jnp.concatenate(k_parts, axis=0).astype(BF16)))
        h //= 2
    small_levels = []
    while h >= 1:
        e = jnp.exp2(_small_level_exponent(h, b2, lf2, row_k))
        upper = (row_k & h) != 0
        small_levels.append((h, jnp.where(upper, q * e, 0.0).astype(BF16),
                             jnp.where(upper, 0.0, k * e).astype(BF16)))
        h //= 2
    qk = q * k
    xor = row ^ col
    row_up = lax.broadcasted_iota(jnp.int32, (CHUNK // 2, CHUNK), 0)
    col_up = lax.broadcasted_iota(jnp.int32, (CHUNK // 2, CHUNK), 1)

    outs = []
    for hd in range(HG_HEADS):
        ks = slice(hd * HG_DK, (hd + 1) * HG_DK)
        vs = slice(hd * HG_DV, (hd + 1) * HG_DV)
        diag = jnp.sum(qk[:, ks], axis=-1, keepdims=True)
        full = jnp.where(xor == 0, diag, 0.0)
        sc = [full[8 * j:8 * j + 8] for j in range(CHUNK // 8)]
        for (h, qu, kl) in big_levels:
            p = _dot_nt(qu[:, ks], kl[:, ks])
            if 2 * h != CHUNK:
                shift = h.bit_length() - 1
                p = jnp.where((row_up >> shift) == (col_up >> (shift + 1)), p, 0.0)
            for m in range(CHUNK // (2 * h)):
                mid = m * 2 * h + h
                for r in range(h // 8):
                    src = m * h + 8 * r
                    sc[mid // 8 + r] = sc[mid // 8 + r] + p[src:src + 8]
        for (h, ql, kl) in small_levels:
            p = jnp.where(xor < 2 * h, _dot_nt(ql[:, ks], kl[:, ks]), 0.0)
            sc = [sc[j] + p[8 * j:8 * j + 8] for j in range(CHUNK // 8)]
        scores = jnp.concatenate(sc, axis=0).astype(BF16)
        st = st_ref[hd]
        iv = i_b[:, vs]
        o = _dot_nt(q_in[:, ks], st.astype(BF16)) + _dot(scores, iv)
        st_ref[hd] = st * decay_all[:, ks] + _dot_tn(iv, k_out[:, ks])
        outs.append(o)
    return outs


def _mixer_kernel(layer, n_tiles, tiles_per_seq, zero_ref, x_ref, xp_ref, w_in_ref, hg_lb_ref, hg_g_ref,
                  sg_g_ref, sg_b_ref, ws_ref, bsx_ref, wa_ref, wb_ref, wo_ref, g_ref, b_ref, o_ref,
                  zf_ref, zb_ref, sg_ref, st_ref, oa_ref, ub_ref):
    i = pl.program_id(0)
    opaque_zero = zero_ref[0]
    rows = MIX_ROWS
    n_chunks = rows // CHUNK
    col_i, col_g = 2 * HG_K, 2 * HG_K + HG_V
    col_u, col_v = 2 * HG_K + 2 * HG_V, 2 * HG_K + 2 * HG_V + SG_WIDTH
    col_a = 2 * HG_K + 2 * HG_V + 2 * SG_WIDTH

    @pl.when(i == 0)
    def _():
        zf_ref[...] = jnp.zeros(zf_ref.shape, F32)
        zb_ref[...] = jnp.zeros(zb_ref.shape, BF16)

    @pl.when(jnp.maximum(i - 1, 0) % tiles_per_seq == 0)
    def _():
        st_ref[...] = jnp.zeros(st_ref.shape, F32)

    xb = x_ref[...].astype(BF16)

    def project(lo, width):
        zero = None
        for c0 in range(lo, lo + width, PROJ_COLS):
            r = _dot(xb, w_in_ref[:, c0:c0 + PROJ_COLS])
            if c0 < F32_COLS:
                lanes = slice(c0, c0 + PROJ_COLS)
                zf_ref[:, lanes] = r
                z = _zero_after_stores(zf_ref, lanes, opaque_zero)
            else:
                lanes = slice(c0 - F32_COLS, c0 - F32_COLS + PROJ_COLS)
                zb_ref[:, lanes] = r.astype(BF16)
                z = _zero_after_stores(zb_ref, lanes, opaque_zero)
            zero = z if zero is None else zero + z
        return zero

    def zb_cols(lo, width, rs=slice(None)):
        return zb_ref[rs, lo - F32_COLS:lo - F32_COLS + width]

    row = lax.broadcasted_iota(jnp.int32, (CHUNK, CHUNK), 0)
    col = lax.broadcasted_iota(jnp.int32, (CHUNK, CHUNK), 1)
    tri = jnp.where(col <= row, 1.0, 0.0).astype(BF16)
    row_k = lax.broadcasted_iota(jnp.int32, (CHUNK, HG_K), 0)

    u = jax.nn.gelu(zb_cols(col_u, SG_WIDTH).astype(F32))
    zero = project(col_u, SG_WIDTH)
    gv = _add_zero(jax.nn.gelu(zb_cols(col_v, SG_WIDTH).astype(F32)), zero)
    v = _layer_norm(gv, sg_g_ref[...], sg_b_ref[...]).astype(BF16)
    zero = project(col_v, SG_WIDTH)
    bsx = bsx_ref[...]
    for g in range(SG_GROUPS):
        gs = slice(g * SG_GROUP_DIM, (g + 1) * SG_GROUP_DIM)
        ws = jnp.where(col <= row, ws_ref[g], 0.0).astype(BF16)
        for c in range(n_chunks):
            rs = slice(c * CHUNK, (c + 1) * CHUNK)
            mixed = _dot(ws, v[rs, gs]) + bsx[:, gs]
            if g == 0 and c == 0:
                mixed = _add_zero(mixed, zero)
            ub_ref[rs, gs] = (u[rs, gs] * mixed).astype(BF16)

    gate_zeros = []
    for c0 in range(0, 2 * D_MODEL, 2 * PROJ_COLS):
        sg_ref[:, c0:c0 + 2 * PROJ_COLS] = jax.nn.sigmoid(zb_cols(col_a + c0, 2 * PROJ_COLS).astype(F32))
        gate_zeros += [project(col_a + c0, PROJ_COLS), project(col_a + c0 + PROJ_COLS, PROJ_COLS)]

    lbs = hg_lb_ref[...]
    ex = jnp.exp(lbs - jnp.max(lbs, axis=0, keepdims=True))
    soft = ex / jnp.sum(ex, axis=0, keepdims=True)
    lb = jnp.zeros((1, HG_K), F32)
    for j in range(1, layer + 1):
        lb = lb + soft[j:j + 1, :]
    hg_g = hg_g_ref[...]

    for c in range(n_chunks):
        rs = slice(c * CHUNK, (c + 1) * CHUNK)
        zq = zf_ref[rs, 0:HG_K]
        zf = zf_ref[rs, HG_K:2 * HG_K]
        f = lb + (1.0 - lb) * jax.nn.sigmoid(zf)
        lf2 = jnp.log(f) * LOG2_E
        kk = (1.0 - lb) * jax.nn.sigmoid(-zf)
        kk = _add_zero(kk, gate_zeros[2 * c])
        i_b = zb_cols(col_i, HG_V, rs)
        zg = zb_cols(col_g, HG_V, rs).astype(F32)
        gate = zg * jax.nn.sigmoid(zg)
        outs = _hgrn_chunk(zq, kk, i_b, lf2, st_ref, tri, row_k, row, col)
        for hd in range(HG_HEADS):
            vs = slice(hd * HG_DV, (hd + 1) * HG_DV)
            o = outs[hd]
            if hd == HG_HEADS - 1:
                o = _add_zero(o, gate_zeros[2 * c + 1])
            o = o * lax.rsqrt(jnp.mean(o * o, axis=-1, keepdims=True) + RMS_EPS) * hg_g[:, vs]
            oa_ref[rs, vs] = (o * gate[:, vs]).astype(BF16)

    y_a = _dot(oa_ref[...], wa_ref[...])
    y_b = _dot(ub_ref[...], wb_ref[...])
    zero = project(0, F32_COLS)
    y = sg_ref[:, 0:D_MODEL] * y_a + sg_ref[:, D_MODEL:2 * D_MODEL] * y_b
    m = _dot(_add_zero(y, zero).astype(BF16), wo_ref[...])
    zero = project(col_i, 2 * HG_V)
    o_ref[...] = _add_zero(_layer_norm(ALPHA * xp_ref[...] + m, g_ref[...], b_ref[...]), zero)


def _mixer(x, layer, w_in, hg_lb, hg_norm_g, sg_ln_g, sg_ln_b, sg_ws, bsx, w_a, w_b, w_out, ln_g, ln_b, seq):
    n = x.shape[0]
    n_tiles = n // MIX_ROWS
    const = dict(pipeline_mode=pl.Buffered(1))

    def vec(width):
        return pl.BlockSpec((None, 1, width), lambda i: (layer, 0, 0), **const)

    def mat(r, c):
        return pl.BlockSpec((None, r, c), lambda i: (layer, 0, 0), **const)

    return pl.pallas_call(
        functools.partial(_mixer_kernel, layer, n_tiles, seq // MIX_ROWS),
        grid=(n_tiles + 1,),
        in_specs=[
            pl.BlockSpec(memory_space=pltpu.SMEM),
            pl.BlockSpec((MIX_ROWS, D_MODEL), lambda i: (jnp.minimum(i, n_tiles - 1), 0)),
            pl.BlockSpec((MIX_ROWS, D_MODEL), lambda i: (jnp.maximum(i - 1, 0), 0)),
            mat(D_MODEL, D_IN),
            pl.BlockSpec((DEPTH, HG_K), lambda i: (0, 0), **const),
            vec(HG_V), vec(SG_WIDTH), vec(SG_WIDTH),
            pl.BlockSpec((None, SG_GROUPS, SG_CHUNK, SG_CHUNK), lambda i: (layer, 0, 0, 0), **const),
            mat(SG_CHUNK, SG_WIDTH),
            mat(HG_V, D_MODEL), mat(SG_WIDTH, D_MODEL), mat(D_MODEL, D_MODEL),
            pl.BlockSpec((None, None, 1, D_MODEL), lambda i: (layer, 1, 0, 0), **const),
            pl.BlockSpec((None, None, 1, D_MODEL), lambda i: (layer, 1, 0, 0), **const),
        ],
        out_specs=pl.BlockSpec((MIX_ROWS, D_MODEL), lambda i: (jnp.maximum(i - 1, 0), 0)),
        out_shape=jax.ShapeDtypeStruct((n, D_MODEL), F32),
        scratch_shapes=[
            pltpu.VMEM((MIX_ROWS, F32_COLS), F32),
            pltpu.VMEM((MIX_ROWS, D_IN - F32_COLS), BF16),
            pltpu.VMEM((MIX_ROWS, 2 * D_MODEL), F32),
            pltpu.VMEM((HG_HEADS, HG_DV, HG_DK), F32),
            pltpu.VMEM((MIX_ROWS, HG_V), BF16),
            pltpu.VMEM((MIX_ROWS, SG_WIDTH), BF16),
        ],
        compiler_params=pltpu.CompilerParams(
            dimension_semantics=("arbitrary",), vmem_limit_bytes=VMEM_LIMIT_BYTES),
        name=f"mixer_l{layer}",
    )(jnp.zeros((1,), jnp.int32), x, x, w_in, hg_lb, hg_norm_g, sg_ln_g, sg_ln_b, sg_ws, bsx, w_a, w_b, w_out,
      ln_g, ln_b)


def kernel(x, ffn1_w13, ffn1_w2, ffn2_w13, ffn2_w2, ln_g, ln_b, w_in, hg_lb, hg_norm_g, sg_ln_g, sg_ln_b,
           sg_ws, sg_bs, w_branch_a, w_branch_b, w_out):
    batch, seq, d = x.shape
    h = x.reshape(batch * seq, d)
    bf = lambda w: w.astype(BF16)
    f1_w13, f1_w2, f2_w13, f2_w2 = bf(ffn1_w13), bf(ffn1_w2), bf(ffn2_w13), bf(ffn2_w2)
    w_in_b, w_a_b, w_b_b, w_out_b = bf(w_in), bf(w_branch_a), bf(w_branch_b), bf(w_out)
    ln_g4 = ln_g.reshape(DEPTH, 3, 1, D_MODEL)
    ln_b4 = ln_b.reshape(DEPTH, 3, 1, D_MODEL)
    hg_g3 = hg_norm_g.reshape(DEPTH, 1, HG_V)
    sg_g3 = sg_ln_g.reshape(DEPTH, 1, SG_WIDTH)
    sg_b3 = sg_ln_b.reshape(DEPTH, 1, SG_WIDTH)
    bsx = jnp.repeat(jnp.swapaxes(sg_bs, 1, 2), SG_GROUP_DIM, axis=2)
    for l in range(DEPTH):
        h = _ffn(h, f1_w13, f1_w2, ln_g4, ln_b4, l, 0)
        h = _mixer(h, l, w_in_b, hg_lb, hg_g3, sg_g3, sg_b3, sg_ws, bsx, w_a_b, w_b_b, w_out_b,
                   ln_g4, ln_b4, seq)
        h = _ffn(h, f2_w13, f2_w2, ln_g4, ln_b4, l, 2)
    return h.reshape(batch, seq, d)
```

```python
import functools

import jax
import jax.numpy as jnp
from jax import lax
from jax.experimental import pallas as pl
from jax.experimental.pallas import tpu as pltpu

D_MODEL = 1024
DEPTH = 4
HG_HEADS = 4
HG_DK = 128
HG_DV = 128
HG_K = HG_HEADS * HG_DK
HG_V = HG_HEADS * HG_DV
SG_GROUPS = 4
SG_WIDTH = 512
SG_GROUP_DIM = SG_WIDTH // SG_GROUPS
SG_CHUNK = 128
D_FF = 2816
ALPHA = (2 * DEPTH) ** 0.25
LN_EPS = 1e-5
RMS_EPS = 1e-6
D_IN = 2 * HG_K + 2 * HG_V + 2 * SG_WIDTH + 2 * D_MODEL
LOG2_E = 1.4426950408889634

CHUNK = 128
FFN_ROWS = 512
FFN_COLS = 256
FFN_NORM_ROWS = 64
MIX_ROWS = 512
PROJ_COLS = 256
F32_COLS = 2 * HG_K
VMEM_LIMIT_BYTES = 56 * 1024 * 1024

F32 = jnp.float32
BF16 = jnp.bfloat16


def _layer_norm(y, g, b):
    mu = jnp.mean(y, axis=-1, keepdims=True)
    d = y - mu
    var = jnp.mean(d * d, axis=-1, keepdims=True)
    return d * lax.rsqrt(var + LN_EPS) * g + b


def _dot(a, b):
    return jnp.dot(a, b, preferred_element_type=F32)


def _dot_nt(a, b):
    return lax.dot_general(a, b, (((1,), (1,)), ((), ())), preferred_element_type=F32)


def _dot_tn(a, b):
    return lax.dot_general(a, b, (((0,), (0,)), ((), ())), preferred_element_type=F32)


def _zero_after_stores(ref, lanes, opaque_zero):
    tile_rows = 8 * (4 // jnp.dtype(ref.dtype).itemsize)
    row = pl.multiple_of(opaque_zero * tile_rows, tile_rows)
    bits = pltpu.bitcast(ref[pl.ds(row, tile_rows), lanes].astype(F32), jnp.uint32)
    acc = bits[0:8, 0:128]
    for r in range(tile_rows // 8):
        for t in range(bits.shape[1] // 128):
            if r or t:
                acc = acc | bits[8 * r:8 * r + 8, t * 128:(t + 1) * 128]
    return pltpu.bitcast((acc >> 16) >> 16, F32)


def _add_zero(v, zero):
    if zero is None:
        return v
    z = zero if v.shape[1] == 128 else jnp.concatenate([zero] * (v.shape[1] // 128), axis=1)
    return jnp.concatenate([v[:8] + z, v[8:]], axis=0)


def _cast_ahead(weights, layer, steps):
    in_specs, out_specs, out_shapes = [], [], []
    for w in weights:
        _, r, c = w.shape
        rows = next(b for b in range(16, r + 1, 16) if r % b == 0 and r // b <= steps)
        last = r // rows - 1
        in_specs.append(pl.BlockSpec((None, rows, c), lambda i, last=last: (layer, jnp.minimum(i, last), 0)))
        out_specs.append(pl.BlockSpec((rows, c), lambda i, last=last: (jnp.minimum(i, last), 0)))
        out_shapes.append(jax.ShapeDtypeStruct((r, c), BF16))
    return in_specs, out_specs, out_shapes


def _split_refs(refs, n_cast):
    return refs[:n_cast], refs[n_cast], refs[n_cast + 1:2 * n_cast + 1], refs[2 * n_cast + 1:]


def _convert_blocks(cast_in, cast_out, opaque_zero):
    zero = None
    for src, dst in zip(cast_in, cast_out):
        dst[...] = src[...].astype(BF16)
        z = _zero_after_stores(dst, slice(None), opaque_zero)
        zero = z if zero is None else zero + z
    return zero


def _ffn_kernel(n_tiles, n_cast, zero_ref, x_ref, w13_ref, w2_ref, g_ref, b_ref, *refs):
    cast_in, o_ref, cast_out, (y_ref,) = _split_refs(refs, n_cast)
    i = pl.program_id(0)
    n_chunks = D_FF // FFN_COLS

    def norm_rows(rs):
        o_ref[rs, :] = _layer_norm(y_ref[rs, :], g_ref[...], b_ref[...])

    @pl.when(i == 0)
    def _():
        y_ref[...] = jnp.zeros(y_ref.shape, F32)

    @pl.when(i < n_tiles)
    def _():
        x = x_ref[...]
        xb = x.astype(BF16)
        acc = jnp.zeros(x.shape, F32)
        after_norm = _convert_blocks(cast_in, cast_out, zero_ref[0])
        for j in range(n_chunks):
            lo = j * FFN_COLS
            a = _dot(xb, w13_ref[:, lo:lo + FFN_COLS])
            b = _dot(xb, w13_ref[:, D_FF + lo:D_FF + lo + FFN_COLS])
            h = (a * jax.nn.sigmoid(a)) * b
            h = _add_zero(h, after_norm)
            after_norm = None
            acc = acc + _dot(h.astype(BF16), w2_ref[lo:lo + FFN_COLS, :])
            if 1 <= j <= FFN_ROWS // FFN_NORM_ROWS:
                norm_rows(slice((j - 1) * FFN_NORM_ROWS, j * FFN_NORM_ROWS))
                after_norm = _zero_after_stores(o_ref, slice(None), zero_ref[0])
        y_ref[...] = ALPHA * x + 0.5 * acc

    @pl.when(i == n_tiles)
    def _():
        norm_rows(slice(None))


def _ffn(x, w13, w2, ln_g, ln_b, layer, which, cast_layer=0, cast_weights=()):
    n = x.shape[0]
    n_tiles = n // FFN_ROWS
    const = dict(pipeline_mode=pl.Buffered(1))
    cast_in, cast_out, cast_shapes = _cast_ahead(cast_weights, cast_layer, n_tiles + 1)
    out = pl.pallas_call(
        functools.partial(_ffn_kernel, n_tiles, len(cast_weights)),
        grid=(n_tiles + 1,),
        in_specs=[
            pl.BlockSpec(memory_space=pltpu.SMEM),
            pl.BlockSpec((FFN_ROWS, D_MODEL), lambda i: (jnp.minimum(i, n_tiles - 1), 0)),
            pl.BlockSpec((D_MODEL, 2 * D_FF), lambda i: (0, 0), **const),
            pl.BlockSpec((D_FF, D_MODEL), lambda i: (0, 0), **const),
            pl.BlockSpec((None, None, 1, D_MODEL), lambda i: (layer, which, 0, 0), **const),
            pl.BlockSpec((None, None, 1, D_MODEL), lambda i: (layer, which, 0, 0), **const),
        ] + cast_in,
        out_specs=[pl.BlockSpec((FFN_ROWS, D_MODEL), lambda i: (jnp.maximum(i - 1, 0), 0))] + cast_out,
        out_shape=[jax.ShapeDtypeStruct((n, D_MODEL), F32)] + cast_shapes,
        scratch_shapes=[pltpu.VMEM((FFN_ROWS, D_MODEL), F32)],
        compiler_params=pltpu.CompilerParams(
            dimension_semantics=("arbitrary",), vmem_limit_bytes=VMEM_LIMIT_BYTES),
        name=f"ffn{which}_l{layer}",
    )(jnp.zeros((1,), jnp.int32), x, w13, w2, ln_g, ln_b, *cast_weights)
    return out[0], out[1:]


def _small_level_exponent(h, b2, lf2, row):
    if h == 1:
        return jnp.where((row & 1) == 1, lf2, 0.0)
    if h == 2:
        nxt = pltpu.roll(lf2, CHUNK - 1, 0)
        prv = pltpu.roll(lf2, 1, 0)
        r = row & 3
        return jnp.where(r == 0, nxt, jnp.where(r == 2, lf2, jnp.where(r == 3, lf2 + prv, 0.0)))
    width = b2.shape[1]
    pieces = []
    for m in range(CHUNK // (2 * h)):
        mid = m * 2 * h + h
        pieces.append(jnp.broadcast_to(b2[mid - 1:mid, :], (2 * h, width)))
    return -jnp.abs(b2 - jnp.concatenate(pieces, axis=0))


def _hgrn_chunk(q, k, i_b, lf2, st_ref, tri, row_k, row, col):
    width = q.shape[1]
    lf_hi = lf2.astype(BF16)
    lf_lo = (lf2 - lf_hi.astype(F32)).astype(BF16)
    b2 = _dot(tri, lf_hi) + _dot(tri, lf_lo)
    b_last = b2[CHUNK - 1:CHUNK, :]
    q_in = (q * jnp.exp2(b2)).astype(BF16)
    k_out = (k * jnp.exp2(b_last - b2)).astype(BF16)
    decay_all = jnp.exp2(b_last)

    big_levels = []
    h = CHUNK // 2
    while h >= 8:
        q_parts, k_parts = [], []
        for m in range(CHUNK // (2 * h)):
            base, mid, top = m * 2 * h, m * 2 * h + h, (m + 1) * 2 * h
            ref = b2[mid - 1:mid, :]
            k_parts.append(k[base:mid] * jnp.exp2(ref - b2[base:mid]))
            k_parts.append(jnp.zeros((h, width), F32))
            q_parts.append(q[mid:top] * jnp.exp2(b2[mid:top] - ref))
        big_levels.append((h, jnp.concatenate(q_parts, axis=0).astype(BF16) if len(q_parts) > 1
                           else q_parts[0].astype(BF16),
                           jnp.concatenate(k_parts, axis=0).astype(BF16)))
        h //= 2
    small_levels = []
    while h >= 1:
        e = jnp.exp2(_small_level_exponent(h, b2, lf2, row_k))
        upper = (row_k & h) != 0
        small_levels.append((h, jnp.where(upper, q * e, 0.0).astype(BF16),
                             jnp.where(upper, 0.0, k * e).astype(BF16)))
        h //= 2
    qk = q * k
    xor = row ^ col
    row_up = lax.broadcasted_iota(jnp.int32, (CHUNK // 2, CHUNK), 0)
    col_up = lax.broadcasted_iota(jnp.int32, (CHUNK // 2, CHUNK), 1)

    outs = []
    for hd in range(HG_HEADS):
        ks = slice(hd * HG_DK, (hd + 1) * HG_DK)
        vs = slice(hd * HG_DV, (hd + 1) * HG_DV)
        diag = jnp.sum(qk[:, ks], axis=-1, keepdims=True)
        full = jnp.where(xor == 0, diag, 0.0)
        sc = [full[8 * j:8 * j + 8] for j in range(CHUNK // 8)]
        for (h, qu, kl) in big_levels:
            p = _dot_nt(qu[:, ks], kl[:, ks])
            if 2 * h != CHUNK:
                shift = h.bit_length() - 1
                p = jnp.where((row_up >> shift) == (col_up >> (shift + 1)), p, 0.0)
            for m in range(CHUNK // (2 * h)):
                mid = m * 2 * h + h
                for r in range(h // 8):
                    src = m * h + 8 * r
                    sc[mid // 8 + r] = sc[mid // 8 + r] + p[src:src + 8]
        for (h, ql, kl) in small_levels:
            p = jnp.where(xor < 2 * h, _dot_nt(ql[:, ks], kl[:, ks]), 0.0)
            sc = [sc[j] + p[8 * j:8 * j + 8] for j in range(CHUNK // 8)]
        scores = jnp.concatenate(sc, axis=0).astype(BF16)
        st = st_ref[hd]
        iv = i_b[:, vs]
        o = _dot_nt(q_in[:, ks], st.astype(BF16)) + _dot(scores, iv)
        st_ref[hd] = st * decay_all[:, ks] + _dot_tn(iv, k_out[:, ks])
        outs.append(o)
    return outs


def _mixer_kernel(layer, n_tiles, tiles_per_seq, zero_ref, x_ref, xp_ref, w_in_ref, hg_lb_ref, hg_g_ref,
                  sg_g_ref, sg_b_ref, ws_ref, bsx_ref, wa_ref, wb_ref, wo_ref, g_ref, b_ref, *refs):
    cast_in, o_ref, cast_out, (zf_ref, zb_ref, sg_ref, st_ref, oa_ref, ub_ref) = _split_refs(refs, 2)
    i = pl.program_id(0)
    opaque_zero = zero_ref[0]
    rows = MIX_ROWS
    n_chunks = rows // CHUNK
    col_i, col_g = 2 * HG_K, 2 * HG_K + HG_V
    col_u, col_v = 2 * HG_K + 2 * HG_V, 2 * HG_K + 2 * HG_V + SG_WIDTH
    col_a = 2 * HG_K + 2 * HG_V + 2 * SG_WIDTH

    @pl.when(i == 0)
    def _():
        zf_ref[...] = jnp.zeros(zf_ref.shape, F32)
        zb_ref[...] = jnp.zeros(zb_ref.shape, BF16)

    @pl.when(jnp.maximum(i - 1, 0) % tiles_per_seq == 0)
    def _():
        st_ref[...] = jnp.zeros(st_ref.shape, F32)

    xb = x_ref[...].astype(BF16)

    def project(lo, width):
        zero = None
        for c0 in range(lo, lo + width, PROJ_COLS):
            r = _dot(xb, w_in_ref[:, c0:c0 + PROJ_COLS])
            if c0 < F32_COLS:
                lanes = slice(c0, c0 + PROJ_COLS)
                zf_ref[:, lanes] = r
                z = _zero_after_stores(zf_ref, lanes, opaque_zero)
            else:
                lanes = slice(c0 - F32_COLS, c0 - F32_COLS + PROJ_COLS)
                zb_ref[:, lanes] = r.astype(BF16)
                z = _zero_after_stores(zb_ref, lanes, opaque_zero)
            zero = z if zero is None else zero + z
        return zero

    def zb_cols(lo, width, rs=slice(None)):
        return zb_ref[rs, lo - F32_COLS:lo - F32_COLS + width]

    row = lax.broadcasted_iota(jnp.int32, (CHUNK, CHUNK), 0)
    col = lax.broadcasted_iota(jnp.int32, (CHUNK, CHUNK), 1)
    tri = jnp.where(col <= row, 1.0, 0.0).astype(BF16)
    row_k = lax.broadcasted_iota(jnp.int32, (CHUNK, HG_K), 0)

    u = jax.nn.gelu(zb_cols(col_u, SG_WIDTH).astype(F32))
    zero = project(col_u, SG_WIDTH) + _convert_blocks(cast_in, cast_out, opaque_zero)
    gv = _add_zero(jax.nn.gelu(zb_cols(col_v, SG_WIDTH).astype(F32)), zero)
    v = _layer_norm(gv, sg_g_ref[...], sg_b_ref[...]).astype(BF16)
    zero = project(col_v, SG_WIDTH)
    bsx = bsx_ref[...]
    for g in range(SG_GROUPS):
        gs = slice(g * SG_GROUP_DIM, (g + 1) * SG_GROUP_DIM)
        ws = jnp.where(col <= row, ws_ref[g], 0.0).astype(BF16)
        for c in range(n_chunks):
            rs = slice(c * CHUNK, (c + 1) * CHUNK)
            mixed = _dot(ws, v[rs, gs]) + bsx[:, gs]
            if g == 0 and c == 0:
                mixed = _add_zero(mixed, zero)
            ub_ref[rs, gs] = (u[rs, gs] * mixed).astype(BF16)

    gate_zeros = []
    for c0 in range(0, 2 * D_MODEL, 2 * PROJ_COLS):
        sg_ref[:, c0:c0 + 2 * PROJ_COLS] = jax.nn.sigmoid(zb_cols(col_a + c0, 2 * PROJ_COLS).astype(F32))
        gate_zeros += [project(col_a + c0, PROJ_COLS), project(col_a + c0 + PROJ_COLS, PROJ_COLS)]

    lbs = hg_lb_ref[...]
    ex = jnp.exp(lbs - jnp.max(lbs, axis=0, keepdims=True))
    soft = ex / jnp.sum(ex, axis=0, keepdims=True)
    lb = jnp.zeros((1, HG_K), F32)
    for j in range(1, layer + 1):
        lb = lb + soft[j:j + 1, :]
    hg_g = hg_g_ref[...]

    for c in range(n_chunks):
        rs = slice(c * CHUNK, (c + 1) * CHUNK)
        zq = zf_ref[rs, 0:HG_K]
        zf = zf_ref[rs, HG_K:2 * HG_K]
        f = lb + (1.0 - lb) * jax.nn.sigmoid(zf)
        lf2 = jnp.log(f) * LOG2_E
        kk = (1.0 - lb) * jax.nn.sigmoid(-zf)
        kk = _add_zero(kk, gate_zeros[2 * c])
        i_b = zb_cols(col_i, HG_V, rs)
        zg = zb_cols(col_g, HG_V, rs).astype(F32)
        gate = zg * jax.nn.sigmoid(zg)
        outs = _hgrn_chunk(zq, kk, i_b, lf2, st_ref, tri, row_k, row, col)
        for hd in range(HG_HEADS):
            vs = slice(hd * HG_DV, (hd + 1) * HG_DV)
            o = outs[hd]
            if hd == HG_HEADS - 1:
                o = _add_zero(o, gate_zeros[2 * c + 1])
            o = o * lax.rsqrt(jnp.mean(o * o, axis=-1, keepdims=True) + RMS_EPS) * hg_g[:, vs]
            oa_ref[rs, vs] = (o * gate[:, vs]).astype(BF16)

    y_a = _dot(oa_ref[...], wa_ref[...])
    y_b = _dot(ub_ref[...], wb_ref[...])
    zero = project(0, F32_COLS)
    y = sg_ref[:, 0:D_MODEL] * y_a + sg_ref[:, D_MODEL:2 * D_MODEL] * y_b
    m = _dot(_add_zero(y, zero).astype(BF16), wo_ref[...])
    zero = project(col_i, 2 * HG_V)
    o_ref[...] = _add_zero(_layer_norm(ALPHA * xp_ref[...] + m, g_ref[...], b_ref[...]), zero)


def _mixer(x, layer, w_in, hg_lb, hg_norm_g, sg_ln_g, sg_ln_b, sg_ws, bsx, w_a, w_b, w_out, ln_g, ln_b, seq,
           cast_weights):
    n = x.shape[0]
    n_tiles = n // MIX_ROWS
    const = dict(pipeline_mode=pl.Buffered(1))
    cast_in, cast_out, cast_shapes = _cast_ahead(cast_weights, layer, n_tiles + 1)

    def vec(width):
        return pl.BlockSpec((None, 1, width), lambda i: (layer, 0, 0), **const)

    def mat(r, c):
        return pl.BlockSpec((None, r, c), lambda i: (layer, 0, 0), **const)

    def whole(r, c):
        return pl.BlockSpec((r, c), lambda i: (0, 0), **const)

    out = pl.pallas_call(
        functools.partial(_mixer_kernel, layer, n_tiles, seq // MIX_ROWS),
        grid=(n_tiles + 1,),
        in_specs=[
            pl.BlockSpec(memory_space=pltpu.SMEM),
            pl.BlockSpec((MIX_ROWS, D_MODEL), lambda i: (jnp.minimum(i, n_tiles - 1), 0)),
            pl.BlockSpec((MIX_ROWS, D_MODEL), lambda i: (jnp.maximum(i - 1, 0), 0)),
            whole(D_MODEL, D_IN),
            pl.BlockSpec((DEPTH, HG_K), lambda i: (0, 0), **const),
            vec(HG_V), vec(SG_WIDTH), vec(SG_WIDTH),
            pl.BlockSpec((None, SG_GROUPS, SG_CHUNK, SG_CHUNK), lambda i: (layer, 0, 0, 0), **const),
            mat(SG_CHUNK, SG_WIDTH),
            whole(HG_V, D_MODEL), whole(SG_WIDTH, D_MODEL), whole(D_MODEL, D_MODEL),
            pl.BlockSpec((None, None, 1, D_MODEL), lambda i: (layer, 1, 0, 0), **const),
            pl.BlockSpec((None, None, 1, D_MODEL), lambda i: (layer, 1, 0, 0), **const),
        ] + cast_in,
        out_specs=[pl.BlockSpec((MIX_ROWS, D_MODEL), lambda i: (jnp.maximum(i - 1, 0), 0))] + cast_out,
        out_shape=[jax.ShapeDtypeStruct((n, D_MODEL), F32)] + cast_shapes,
        scratch_shapes=[
            pltpu.VMEM((MIX_ROWS, F32_COLS), F32),
            pltpu.VMEM((MIX_ROWS, D_IN - F32_COLS), BF16),
            pltpu.VMEM((MIX_ROWS, 2 * D_MODEL), F32),
            pltpu.VMEM((HG_HEADS, HG_DV, HG_DK), F32),
            pltpu.VMEM((MIX_ROWS, HG_V), BF16),
            pltpu.VMEM((MIX_ROWS, SG_WIDTH), BF16),
        ],
        compiler_params=pltpu.CompilerParams(
            dimension_semantics=("arbitrary",), vmem_limit_bytes=VMEM_LIMIT_BYTES),
        name=f"mixer_l{layer}",
    )(jnp.zeros((1,), jnp.int32), x, x, w_in, hg_lb, hg_norm_g, sg_ln_g, sg_ln_b, sg_ws, bsx, w_a, w_b, w_out,
      ln_g, ln_b, *cast_weights)
    return out[0], out[1:]


def kernel(x, ffn1_w13, ffn1_w2, ffn2_w13, ffn2_w2, ln_g, ln_b, w_in, hg_lb, hg_norm_g, sg_ln_g, sg_ln_b,
           sg_ws, sg_bs, w_branch_a, w_branch_b, w_out):
    batch, seq, d = x.shape
    h = x.reshape(batch * seq, d)
    ln_g4 = ln_g.reshape(DEPTH, 3, 1, D_MODEL)
    ln_b4 = ln_b.reshape(DEPTH, 3, 1, D_MODEL)
    hg_g3 = hg_norm_g.reshape(DEPTH, 1, HG_V)
    sg_g3 = sg_ln_g.reshape(DEPTH, 1, SG_WIDTH)
    sg_b3 = sg_ln_b.reshape(DEPTH, 1, SG_WIDTH)
    bsx = jnp.repeat(jnp.swapaxes(sg_bs, 1, 2), SG_GROUP_DIM, axis=2)
    w13_b, w2_b = ffn1_w13[0].astype(BF16), ffn1_w2[0].astype(BF16)
    for l in range(DEPTH):
        h, (w_in_b, w_a_b, w_b_b, w_out_b) = _ffn(
            h, w13_b, w2_b, ln_g4, ln_b4, l, 0, l, (w_in, w_branch_a, w_branch_b, w_out))
        h, (w13_b, w2_b) = _mixer(h, l, w_in_b, hg_lb, hg_g3, sg_g3, sg_b3, sg_ws, bsx, w_a_b, w_b_b, w_out_b,
                                  ln_g4, ln_b4, seq, (ffn2_w13, ffn2_w2))
        nxt = (ffn1_w13, ffn1_w2) if l + 1 < DEPTH else ()
        h, casted = _ffn(h, w13_b, w2_b, ln_g4, ln_b4, l, 2, l + 1, nxt)
        if nxt:
            w13_b, w2_b = casted
    return h.reshape(batch, seq, d)
```
